```python
import math
import jax, jax.numpy as jnp
from jax import lax
import numpy as np


D_MODEL = 1024
BATCH = 8
SEQ = 4096
DEPTH = 1

CHUNK = 64
Q_BLOCK = 128
D_CONV = 512
CONV_K = 3
N_HEADS = 4
HEAD_DIM = 64
V_HEAD_DIM = 2 * HEAD_DIM
ATTN_QK = N_HEADS * 2 * HEAD_DIM
ATTN_V = N_HEADS * V_HEAD_DIM
D_FF = int(math.ceil(8 * D_MODEL / 3 / 256)) * 256
ROPE_THETA = 10000.0
EPS = 1e-6
SPLITS = (D_CONV, D_CONV, D_CONV, ATTN_QK, ATTN_QK, ATTN_V, D_MODEL, D_MODEL)
IN_COLS = sum(SPLITS)
SPLIT_IDX = [int(v) for v in np.cumsum(SPLITS)[:-1]]

kernel_name = "hybrid_shortconv_diffattn_block"


def _rmsnorm(x, g):
    xf = x.astype(jnp.float32)
    y = xf * lax.rsqrt(jnp.mean(xf * xf, axis=-1, keepdims=True) + EPS)
    return (y * g.astype(jnp.float32)).astype(x.dtype)


def _rope_tables(seq):
    pos = jnp.arange(seq, dtype=jnp.float32)
    inv = 1.0 / (ROPE_THETA ** (jnp.arange(0, HEAD_DIM, 2, dtype=jnp.float32) / HEAD_DIM))
    ang = pos[:, None] * inv[None, :]
    ang = jnp.concatenate([ang, ang], axis=-1)
    return jnp.cos(ang), jnp.sin(ang)


def _apply_rope(x, cos, sin):
    xf = x.astype(jnp.float32)
    x1, x2 = jnp.split(xf, 2, axis=-1)
    rot = jnp.concatenate([-x2, x1], axis=-1)
    c = cos[:, None, None, :]
    s = sin[:, None, None, :]
    return (xf * c + rot * s).astype(x.dtype)


def _short_conv_mixer(b_gate, c_gate, v, conv_w):
    u = c_gate * v
    y = lax.conv_general_dilated(
        u, conv_w[:, None, :].astype(u.dtype), window_strides=(1,),
        padding=[(CONV_K - 1, 0)], dimension_numbers=('NWC', 'WIO', 'NWC'),
        feature_group_count=D_CONV)
    return b_gate * y


def _diff_attention(q, k, v, q_norm, k_norm, lq1, lk1, lq2, lk2, sub_norm, lam_init, cos, sin):
    B, S = q.shape[0], q.shape[1]
    q = q.reshape(B, S, N_HEADS, 2, HEAD_DIM)
    k = k.reshape(B, S, N_HEADS, 2, HEAD_DIM)
    q = _apply_rope(_rmsnorm(q, q_norm), cos, sin)
    k = _apply_rope(_rmsnorm(k, k_norm), cos, sin)
    f32 = jnp.float32
    lam = (jnp.exp(jnp.sum(lq1.astype(f32) * lk1.astype(f32)))
           - jnp.exp(jnp.sum(lq2.astype(f32) * lk2.astype(f32))) + lam_init)
    q = q.transpose(3, 0, 2, 1, 4)
    k = k.transpose(3, 0, 2, 1, 4)
    vh = v.reshape(B, S, N_HEADS, V_HEAD_DIM).transpose(0, 2, 1, 3)
    nb = S // Q_BLOCK
    qb = q.reshape(2, B, N_HEADS, nb, Q_BLOCK, HEAD_DIM).transpose(3, 0, 1, 2, 4, 5)
    key_chunk = jnp.arange(S) // CHUNK
    scale = HEAD_DIM ** -0.5

    def block(args):
        qblk, i = args
        q_chunk = (i * Q_BLOCK + jnp.arange(Q_BLOCK)) // CHUNK
        mask = key_chunk[None, :] <= q_chunk[:, None]
        s = jnp.einsum('nbhqd,nbhkd->nbhqk', qblk, k).astype(f32) * scale
        p = jax.nn.softmax(jnp.where(mask, s, -jnp.inf), axis=-1)
        w = (p[0] - lam * p[1]).astype(vh.dtype)
        return jnp.einsum('bhqk,bhkd->bhqd', w, vh)

    o = lax.map(block, (qb, jnp.arange(nb)))
    o = o.transpose(1, 0, 3, 2, 4).reshape(B, S, N_HEADS, V_HEAD_DIM)
    o = _rmsnorm(o, sub_norm) * (1.0 - lam_init)
    return o.reshape(B, S, ATTN_V)


def setup_inputs(seed: int = 0) -> dict:
    key = jax.random.key(seed)
    ks = jax.random.split(key, 20)
    n = jax.random.normal
    f = jnp.float32
    L = DEPTH
    return {
        "x": n(ks[0], (BATCH, SEQ, D_MODEL), f),
        "g_mix": 1.0 + 0.02 * n(ks[1], (L, D_MODEL), f),
        "w_in": n(ks[2], (L, D_MODEL, IN_COLS), f) * D_MODEL ** -0.5,
        "b_gate": 0.02 * n(ks[3], (L, 2 * D_MODEL), f),
        "conv_w": n(ks[4], (L, CONV_K, D_CONV), f) * CONV_K ** -0.5,
        "q_norm": 1.0 + 0.02 * n(ks[5], (L, HEAD_DIM), f),
        "k_norm": 1.0 + 0.02 * n(ks[6], (L, HEAD_DIM), f),
        "lambda_q1": 0.1 * n(ks[7], (L, HEAD_DIM), f),
        "lambda_k1": 0.1 * n(ks[8], (L, HEAD_DIM), f),
        "lambda_q2": 0.1 * n(ks[9], (L, HEAD_DIM), f),
        "lambda_k2": 0.1 * n(ks[10], (L, HEAD_DIM), f),
        "sub_norm": 1.0 + 0.02 * n(ks[11], (L, V_HEAD_DIM), f),
        "w_conv_out": n(ks[12], (L, D_CONV, D_MODEL), f) * D_CONV ** -0.5,
        "w_attn_out": n(ks[13], (L, ATTN_V, D_MODEL), f) * ATTN_V ** -0.5,
        "w_o": n(ks[14], (L, D_MODEL, D_MODEL), f) * D_MODEL ** -0.5,
        "g_ffn": 1.0 + 0.02 * n(ks[15], (L, D_MODEL), f),
        "w_gate_up": n(ks[16], (L, D_MODEL, 2 * D_FF), f) * D_MODEL ** -0.5,
        "w_down": n(ks[17], (L, D_FF, D_MODEL), f) * D_FF ** -0.5,
    }


def reference(x, g_mix, w_in, b_gate, conv_w, q_norm, k_norm, lambda_q1, lambda_k1,
              lambda_q2, lambda_k2, sub_norm, w_conv_out, w_attn_out, w_o, g_ffn,
              w_gate_up, w_down):
    S = x.shape[1]
    cos, sin = _rope_tables(S)
    for l in range(DEPTH):
        lam_init = 0.8 - 0.6 * math.exp(-0.3 * l)
        h = _rmsnorm(x, g_mix[l])
        z = jnp.einsum('bsd,dc->bsc', h, w_in[l])
        bc, cc, vc, q, k, va, gc_pre, ga_pre = jnp.split(z, SPLIT_IDX, axis=-1)
        g_c = jax.nn.sigmoid(gc_pre + b_gate[l][:D_MODEL])
        g_a = jax.nn.sigmoid(ga_pre + b_gate[l][D_MODEL:])
        y_c = _short_conv_mixer(bc, cc, vc, conv_w[l]) @ w_conv_out[l]
        y_a = _diff_attention(q, k, va, q_norm[l], k_norm[l], lambda_q1[l], lambda_k1[l],
                              lambda_q2[l], lambda_k2[l], sub_norm[l], lam_init, cos, sin) @ w_attn_out[l]
        x = x + (g_c * y_c + g_a * y_a) @ w_o[l]
        h2 = _rmsnorm(x, g_ffn[l])
        gt, up = jnp.split(h2 @ w_gate_up[l], 2, axis=-1)
        x = x + (jax.nn.silu(gt) * up) @ w_down[l]
    return x
```

```python
import functools
import math

import jax
import jax.numpy as jnp
from jax import lax
from jax.experimental import pallas as pl
from jax.experimental.pallas import tpu as pltpu

D_MODEL = 1024
CHUNK = 64
D_CONV = 512
CONV_K = 3
N_HEADS = 4
HEAD_DIM = 64
V_HEAD_DIM = 2 * HEAD_DIM
ATTN_QK = N_HEADS * 2 * HEAD_DIM
ATTN_V = N_HEADS * V_HEAD_DIM
D_FF = 2816
ROPE_THETA = 10000.0
EPS = 1e-6
LAM_INIT = 0.8 - 0.6 * math.exp(-0.3 * 0)

N_GROUPS = ATTN_QK // HEAD_DIM
HALF = HEAD_DIM // 2
QKV_COLS = ATTN_QK + ATTN_QK + ATTN_V
CONV_COLS = 3 * D_CONV

TS_PROJ = 512
TQ = 256
TK = 256
TS_POST = 512
FF_CHUNKS = ((0, 1024), (1024, 2048), (2048, 2816))
NEG_BIG = -1e30
VMEM_LIMIT = 56 * 1024 * 1024

_NT = (((1,), (1,)), ((), ()))


def _rms_rows(x, g):
    ms = jnp.mean(x * x, axis=-1, keepdims=True)
    return x * lax.rsqrt(ms + EPS) * g


def _norm_rope_t(z, c_tab, s_tab):
    r = lax.rsqrt(jnp.mean(z * z, axis=0, keepdims=True) + EPS)
    swap = jnp.concatenate([z[HALF:], z[:HALF]], axis=0)
    return (z * c_tab + swap * s_tab) * r


def _proj_kernel(x_ref, g_ref, wc_ref, wt_ref, cw_ref, cq_ref, sq_ref, ck_ref, sk_ref,
                 conv_ref, qt_ref, k_ref, vt_ref, carry_ref):
    s_idx = pl.program_id(1)
    ts = x_ref.shape[1]

    h = _rms_rows(x_ref[0], g_ref[...]).astype(jnp.bfloat16)

    zc = jnp.dot(h, wc_ref[...], preferred_element_type=jnp.float32)
    bc = zc[:, :D_CONV]
    u = zc[:, D_CONV:2 * D_CONV] * zc[:, 2 * D_CONV:]

    @pl.when(s_idx == 0)
    def _():
        carry_ref[...] = jnp.zeros_like(carry_ref)

    prev = carry_ref[...]
    row = lax.broadcasted_iota(jnp.int32, u.shape, 0)
    u1 = jnp.where(row == 0, prev[7:8], pltpu.roll(u, 1, 0))
    u2 = jnp.where(row == 0, prev[6:7],
                   jnp.where(row == 1, prev[7:8], pltpu.roll(u, 2, 0)))
    cw = cw_ref[...]
    y = cw[0:1] * u2 + cw[1:2] * u1 + cw[2:3] * u
    conv_ref[0] = (bc * y).astype(conv_ref.dtype)
    carry_ref[...] = u[ts - 8:]

    zt = lax.dot_general(wt_ref[...], h, _NT, preferred_element_type=jnp.float32)
    cq, sq, ck, sk = cq_ref[...], sq_ref[...], ck_ref[...], sk_ref[...]
    for g in range(N_GROUPS):
        lo = g * HEAD_DIM
        qt_ref[0, lo:lo + HEAD_DIM, :] = _norm_rope_t(zt[lo:lo + HEAD_DIM], cq, sq).astype(qt_ref.dtype)
    kt = jnp.concatenate(
        [_norm_rope_t(zt[ATTN_QK + g * HEAD_DIM:ATTN_QK + (g + 1) * HEAD_DIM], ck, sk)
         for g in range(N_GROUPS)], axis=0)
    k_ref[0] = kt.T.astype(k_ref.dtype)
    vt_ref[0] = zt[2 * ATTN_QK:].astype(vt_ref.dtype)


def _proj_call(x, g_mix, w_conv_in, w_qkv_t, conv_w, cq, sq, ck, sk):
    B, S, D = x.shape
    ts = TS_PROJ
    bf = jnp.bfloat16
    const = lambda b, s: (0, 0)
    return pl.pallas_call(
        _proj_kernel,
        grid=(B, S // ts),
        in_specs=[
            pl.BlockSpec((1, ts, D), lambda b, s: (b, s, 0)),
            pl.BlockSpec((1, D), const),
            pl.BlockSpec((D, CONV_COLS), const),
            pl.BlockSpec((QKV_COLS, D), const),
            pl.BlockSpec((CONV_K, D_CONV), const),
            pl.BlockSpec((HEAD_DIM, ts), lambda b, s: (0, s)),
            pl.BlockSpec((HEAD_DIM, ts), lambda b, s: (0, s)),
            pl.BlockSpec((HEAD_DIM, ts), lambda b, s: (0, s)),
            pl.BlockSpec((HEAD_DIM, ts), lambda b, s: (0, s)),
        ],
        out_specs=[
            pl.BlockSpec((1, ts, D_CONV), lambda b, s: (b, s, 0)),
            pl.BlockSpec((1, ATTN_QK, ts), lambda b, s: (b, 0, s)),
            pl.BlockSpec((1, ts, ATTN_QK), lambda b, s: (b, s, 0)),
            pl.BlockSpec((1, ATTN_V, ts), lambda b, s: (b, 0, s)),
        ],
        out_shape=[
            jax.ShapeDtypeStruct((B, S, D_CONV), bf),
            jax.ShapeDtypeStruct((B, ATTN_QK, S), bf),
            jax.ShapeDtypeStruct((B, S, ATTN_QK), bf),
            jax.ShapeDtypeStruct((B, ATTN_V, S), bf),
        ],
        scratch_shapes=[pltpu.VMEM((8, D_CONV), jnp.float32)],
        compiler_params=pltpu.CompilerParams(
            dimension_semantics=("arbitrary", "arbitrary"),
            vmem_limit_bytes=VMEM_LIMIT),
        name="proj",
    )(x, g_mix, w_conv_in, w_qkv_t, conv_w, cq, sq, ck, sk)


def _attn_kernel(qt_ref, k_ref, vt_ref, lq1_ref, lk1_ref, lq2_ref, lk2_ref, gsub_ref, o_ref):
    qi = pl.program_id(2)
    tq = qt_ref.shape[2]
    tk = TK

    qt = qt_ref[0]
    sub_row = lax.broadcasted_iota(jnp.int32, qt.shape, 0)
    zero = jnp.zeros_like(qt)
    qz = (jnp.where(sub_row < HEAD_DIM, qt, zero), jnp.where(sub_row >= HEAD_DIM, qt, zero))

    def step(j, carry, masked):
        off = pl.multiple_of(j * tk, tk)
        kt = k_ref[0, pl.ds(off, tk), :]
        vt = vt_ref[0, :, pl.ds(off, tk)]
        out = []
        for i in range(2):
            m, l, acc = carry[i]
            s = jnp.dot(kt, qz[i], preferred_element_type=jnp.float32)
            if masked:
                kc = lax.broadcasted_iota(jnp.int32, s.shape, 0) // CHUNK
                qc = lax.broadcasted_iota(jnp.int32, s.shape, 1) // CHUNK
                s = jnp.where(kc <= qc, s, NEG_BIG)
            m_new = jnp.maximum(m, jnp.max(s, axis=0, keepdims=True))
            alpha = jnp.exp2(m - m_new)
            p = jnp.exp2(s - m_new)
            l = alpha * l + jnp.sum(p, axis=0, keepdims=True)
            acc = alpha * acc + jnp.dot(vt, p.astype(vt.dtype), preferred_element_type=jnp.float32)
            out.append((m_new, l, acc))
        return tuple(out)

    init = tuple((jnp.full((1, tq), NEG_BIG, jnp.float32),
                  jnp.zeros((1, tq), jnp.float32),
                  jnp.zeros((V_HEAD_DIM, tq), jnp.float32)) for _ in range(2))
    carry = lax.fori_loop(0, qi, lambda j, c: step(j, c, False), init)
    (_, l0, acc0), (_, l1, acc1) = step(qi, carry, True)

    f32 = jnp.float32
    lam = (jnp.exp(jnp.sum(lq1_ref[...].astype(f32) * lk1_ref[...].astype(f32), axis=-1, keepdims=True))
           - jnp.exp(jnp.sum(lq2_ref[...].astype(f32) * lk2_ref[...].astype(f32), axis=-1, keepdims=True))
           + LAM_INIT)
    o = acc0 / l0 - lam * (acc1 / l1)
    on = o * lax.rsqrt(jnp.mean(o * o, axis=0, keepdims=True) + EPS)
    o_ref[0] = ((on.T * gsub_ref[...]) * (1.0 - LAM_INIT)).astype(o_ref.dtype)


def _attn_call(qt, k, vt, lq1, lk1, lq2, lk2, gsub):
    B, _, S = qt.shape
    assert TQ == TK and TQ % CHUNK == 0
    const = lambda b, h, q: (0, 0)
    return pl.pallas_call(
        _attn_kernel,
        grid=(B, N_HEADS, S // TQ),
        in_specs=[
            pl.BlockSpec((1, 2 * HEAD_DIM, TQ), lambda b, h, q: (b, h, q)),
            pl.BlockSpec((1, S, 2 * HEAD_DIM), lambda b, h, q: (b, 0, h)),
            pl.BlockSpec((1, V_HEAD_DIM, S), lambda b, h, q: (b, h, 0)),
            pl.BlockSpec((1, HEAD_DIM), const),
            pl.BlockSpec((1, HEAD_DIM), const),
            pl.BlockSpec((1, HEAD_DIM), const),
            pl.BlockSpec((1, HEAD_DIM), const),
            pl.BlockSpec((1, V_HEAD_DIM), const),
        ],
        out_specs=pl.BlockSpec((1, TQ, V_HEAD_DIM), lambda b, h, q: (b, q, h)),
        out_shape=jax.ShapeDtypeStruct((B, S, ATTN_V), jnp.bfloat16),
        compiler_params=pltpu.CompilerParams(
            dimension_semantics=("arbitrary", "arbitrary", "arbitrary"),
            vmem_limit_bytes=VMEM_LIMIT),
        name="diff_attn",
    )(qt, k, vt, lq1, lk1, lq2, lk2, gsub)


def _post_kernel(x_ref, conv_ref, o_ref, gmix_ref, wg_ref, bg_ref, wco_ref, wao_ref, wo_ref,
                 gffn_ref, wgu_ref, wd_ref, out_ref):
    f32 = jnp.float32
    bf = jnp.bfloat16
    x = x_ref[...]
    h = _rms_rows(x, gmix_ref[...]).astype(bf)
    gates = jax.nn.sigmoid(jnp.dot(h, wg_ref[...], preferred_element_type=f32) + bg_ref[...])
    y_c = jnp.dot(conv_ref[...], wco_ref[...], preferred_element_type=f32)
    y_a = jnp.dot(o_ref[...], wao_ref[...], preferred_element_type=f32)
    mix = (gates[:, :D_MODEL] * y_c + gates[:, D_MODEL:] * y_a).astype(bf)
    x1 = x + jnp.dot(mix, wo_ref[...], preferred_element_type=f32)
    h2 = _rms_rows(x1, gffn_ref[...]).astype(bf)
    acc = x1
    for lo, hi in FF_CHUNKS:
        gt = jnp.dot(h2, wgu_ref[:, lo:hi], preferred_element_type=f32)
        up = jnp.dot(h2, wgu_ref[:, D_FF + lo:D_FF + hi], preferred_element_type=f32)
        a = (gt * jax.nn.sigmoid(gt) * up).astype(bf)
        acc = acc + jnp.dot(a, wd_ref[lo:hi, :], preferred_element_type=f32)
    out_ref[...] = acc


def _post_call(x2, conv2, o2, g_mix, w_gates, b_gate, w_co, w_ao, w_o, g_ffn, w_gu, w_d):
    T, D = x2.shape
    ts = TS_POST
    row = lambda i: (i, 0)
    const = lambda i: (0, 0)

    def resident(shape):
        return pl.BlockSpec(shape, const, pipeline_mode=pl.Buffered(1))

    return pl.pallas_call(
        _post_kernel,
        grid=(T // ts,),
        in_specs=[
            pl.BlockSpec((ts, D), row),
            pl.BlockSpec((ts, D_CONV), row),
            pl.BlockSpec((ts, ATTN_V), row),
            resident((1, D)),
            resident((D, 2 * D)),
            resident((1, 2 * D)),
            resident((D_CONV, D)),
            resident((ATTN_V, D)),
            resident((D, D)),
            resident((1, D)),
            resident((D, 2 * D_FF)),
            resident((D_FF, D)),
        ],
        out_specs=pl.BlockSpec((ts, D), row),
        out_shape=jax.ShapeDtypeStruct((T, D), jnp.float32),
        compiler_params=pltpu.CompilerParams(
            dimension_semantics=("arbitrary",),
            vmem_limit_bytes=VMEM_LIMIT),
        name="post",
    )(x2, conv2, o2, g_mix, w_gates, b_gate, w_co, w_ao, w_o, g_ffn, w_gu, w_d)


def _rope_gain_tables(seq, gain, out_scale):
    pos = jnp.arange(seq, dtype=jnp.float32)
    inv = 1.0 / (ROPE_THETA ** (jnp.arange(0, HEAD_DIM, 2, dtype=jnp.float32) / HEAD_DIM))
    ang = inv[:, None] * pos[None, :]
    ang = jnp.concatenate([ang, ang], axis=0)
    g = gain.astype(jnp.float32)
    g_swap = jnp.concatenate([-g[HALF:], g[:HALF]])
    c_tab = jnp.cos(ang) * (g * out_scale)[:, None]
    s_tab = jnp.sin(ang) * (g_swap * out_scale)[:, None]
    return c_tab, s_tab


def kernel(x, g_mix, w_in, b_gate, conv_w, q_norm, k_norm, lambda_q1, lambda_k1, lambda_q2,
           lambda_k2, sub_norm, w_conv_out, w_attn_out, w_o, g_ffn, w_gate_up, w_down):
    B, S, D = x.shape
    bf = jnp.bfloat16
    l = 0
    q_scale = HEAD_DIM ** -0.5 * math.log2(math.e)
    cq, sq = _rope_gain_tables(S, q_norm[l], q_scale)
    ck, sk = _rope_gain_tables(S, k_norm[l], 1.0)

    w = w_in[l]
    w_conv_in = w[:, :CONV_COLS].astype(bf)
    w_qkv_t = w[:, CONV_COLS:CONV_COLS + QKV_COLS].T.astype(bf)
    w_gates = w[:, CONV_COLS + QKV_COLS:].astype(bf)

    conv_pre, qt, k, vt = _proj_call(x, g_mix[l][None], w_conv_in, w_qkv_t, conv_w[l], cq, sq, ck, sk)
    o = _attn_call(qt, k, vt, lambda_q1[l][None], lambda_k1[l][None], lambda_q2[l][None],
                   lambda_k2[l][None], sub_norm[l][None])
    out = _post_call(
        x.reshape(B * S, D), conv_pre.reshape(B * S, D_CONV), o.reshape(B * S, ATTN_V),
        g_mix[l][None], w_gates, b_gate[l][None], w_conv_out[l].astype(bf), w_attn_out[l].astype(bf),
        w_o[l].astype(bf), g_ffn[l][None], w_gate_up[l].astype(bf), w_down[l].astype(bf))
    return out.reshape(B, S, D)
```

```python
import functools
import math

import jax
import jax.numpy as jnp
from jax import lax
from jax.experimental import pallas as pl
from jax.experimental.pallas import tpu as pltpu

D_MODEL = 1024
CHUNK = 64
D_CONV = 512
CONV_K = 3
N_HEADS = 4
HEAD_DIM = 64
V_HEAD_DIM = 2 * HEAD_DIM
ATTN_QK = N_HEADS * 2 * HEAD_DIM
ATTN_V = N_HEADS * V_HEAD_DIM
D_FF = 2816
ROPE_THETA = 10000.0
EPS = 1e-6
LAM_INIT = 0.8 - 0.6 * math.exp(-0.3 * 0)

N_GROUPS = ATTN_QK // HEAD_DIM
HALF = HEAD_DIM // 2
QKV_COLS = ATTN_QK + ATTN_QK + ATTN_V
CONV_COLS = 3 * D_CONV

TS_PROJ = 512
TQ = 256
TK = 256
TS_POST = 512
FF_CHUNKS = ((0, 1024), (1024, 2048), (2048, 2816))
NEG_BIG = -1e30
VMEM_LIMIT = 56 * 1024 * 1024

_NT = (((1,), (1,)), ((), ()))


def _rms_rows(x, g):
    ms = jnp.mean(x * x, axis=-1, keepdims=True)
    return x * lax.rsqrt(ms + EPS) * g


def _norm_rope_t(z, c_tab, s_tab):
    r = lax.rsqrt(jnp.mean(z * z, axis=0, keepdims=True) + EPS)
    swap = jnp.concatenate([z[HALF:], z[:HALF]], axis=0)
    return (z * c_tab + swap * s_tab) * r


def _proj_kernel(x_ref, g_ref, wc_ref, wt_ref, cw_ref, cq_ref, sq_ref, ck_ref, sk_ref,
                 conv_ref, qt_ref, k_ref, vt_ref, carry_ref):
    s_idx = pl.program_id(1)
    ts = x_ref.shape[1]

    h = _rms_rows(x_ref[0], g_ref[...]).astype(jnp.bfloat16)

    zc = jnp.dot(h, wc_ref[...], preferred_element_type=jnp.float32)
    bc = zc[:, :D_CONV]
    u = zc[:, D_CONV:2 * D_CONV] * zc[:, 2 * D_CONV:]

    @pl.when(s_idx == 0)
    def _():
        carry_ref[...] = jnp.zeros_like(carry_ref)

    prev = carry_ref[...]
    row = lax.broadcasted_iota(jnp.int32, u.shape, 0)
    u1 = jnp.where(row == 0, prev[7:8], pltpu.roll(u, 1, 0))
    u2 = jnp.where(row == 0, prev[6:7],
                   jnp.where(row == 1, prev[7:8], pltpu.roll(u, 2, 0)))
    cw = cw_ref[...]
    y = cw[0:1] * u2 + cw[1:2] * u1 + cw[2:3] * u
    conv_ref[0] = (bc * y).astype(conv_ref.dtype)
    carry_ref[...] = u[ts - 8:]

    zt = lax.dot_general(wt_ref[...], h, _NT, preferred_element_type=jnp.float32)
    cq, sq, ck, sk = cq_ref[...], sq_ref[...], ck_ref[...], sk_ref[...]
    for g in range(N_GROUPS):
        lo = g * HEAD_DIM
        qt_ref[0, lo:lo + HEAD_DIM, :] = _norm_rope_t(zt[lo:lo + HEAD_DIM], cq, sq).astype(qt_ref.dtype)
    kt = jnp.concatenate(
        [_norm_rope_t(zt[ATTN_QK + g * HEAD_DIM:ATTN_QK + (g + 1) * HEAD_DIM], ck, sk)
         for g in range(N_GROUPS)], axis=0)
    k_ref[0] = kt.T.astype(k_ref.dtype)
    vt_ref[0] = zt[2 * ATTN_QK:].astype(vt_ref.dtype)


def _proj_call(x, g_mix, w_conv_in, w_qkv_t, conv_w, cq, sq, ck, sk):
    B, S, D = x.shape
    ts = TS_PROJ
    bf = jnp.bfloat16
    const = lambda b, s: (0, 0)
    return pl.pallas_call(
        _proj_kernel,
        grid=(B, S // ts),
        in_specs=[
            pl.BlockSpec((1, ts, D), lambda b, s: (b, s, 0)),
            pl.BlockSpec((1, D), const),
            pl.BlockSpec((D, CONV_COLS), const),
            pl.BlockSpec((QKV_COLS, D), const),
            pl.BlockSpec((CONV_K, D_CONV), const),
            pl.BlockSpec((HEAD_DIM, ts), lambda b, s: (0, s)),
            pl.BlockSpec((HEAD_DIM, ts), lambda b, s: (0, s)),
            pl.BlockSpec((HEAD_DIM, ts), lambda b, s: (0, s)),
            pl.BlockSpec((HEAD_DIM, ts), lambda b, s: (0, s)),
        ],
        out_specs=[
            pl.BlockSpec((1, ts, D_CONV), lambda b, s: (b, s, 0)),
            pl.BlockSpec((1, ATTN_QK, ts), lambda b, s: (b, 0, s)),
            pl.BlockSpec((1, ts, ATTN_QK), lambda b, s: (b, s, 0)),
            pl.BlockSpec((1, ATTN_V, ts), lambda b, s: (b, 0, s)),
        ],
        out_shape=[
            jax.ShapeDtypeStruct((B, S, D_CONV), bf),
            jax.ShapeDtypeStruct((B, ATTN_QK, S), bf),
            jax.ShapeDtypeStruct((B, S, ATTN_QK), bf),
            jax.ShapeDtypeStruct((B, ATTN_V, S), bf),
        ],
        scratch_shapes=[pltpu.VMEM((8, D_CONV), jnp.float32)],
        compiler_params=pltpu.CompilerParams(
            dimension_semantics=("arbitrary", "arbitrary"),
            vmem_limit_bytes=VMEM_LIMIT),
        name="proj",
    )(x, g_mix, w_conv_in, w_qkv_t, conv_w, cq, sq, ck, sk)


def _attn_kernel(qt_ref, k_ref, vt_ref, lq1_ref, lk1_ref, lq2_ref, lk2_ref, gsub_ref, o_ref,
                 qz_ref, m_ref, l_ref, *acc_refs):
    qi = pl.program_id(1)
    tq = o_ref.shape[1]
    tk = TK
    hw = 2 * HEAD_DIM

    for h in range(N_HEADS):
        qt = qt_ref[0, h * hw:(h + 1) * hw, :]
        sub_row = lax.broadcasted_iota(jnp.int32, qt.shape, 0)
        zero = jnp.zeros_like(qt)
        qz_ref[h] = jnp.concatenate(
            [jnp.where(sub_row < HEAD_DIM, qt, zero), jnp.where(sub_row >= HEAD_DIM, qt, zero)], axis=1)
        acc_refs[h][...] = jnp.zeros_like(acc_refs[h])
    m_ref[...] = jnp.full_like(m_ref, NEG_BIG)
    l_ref[...] = jnp.zeros_like(l_ref)

    def scores(h, off):
        kt = k_ref[0, pl.ds(off, tk), h * hw:(h + 1) * hw]
        return jnp.dot(kt, qz_ref[h], preferred_element_type=jnp.float32)

    def softmax_pv(h, s, off, masked):
        if masked:
            kc = lax.broadcasted_iota(jnp.int32, s.shape, 0) // CHUNK
            qc = (lax.broadcasted_iota(jnp.int32, s.shape, 1) % tq) // CHUNK
            s = jnp.where(kc <= qc, s, NEG_BIG)
        m = m_ref[h]
        m_new = jnp.maximum(m, jnp.max(s, axis=0, keepdims=True))
        alpha = jnp.exp2(m - m_new)
        p = jnp.exp2(s - m_new)
        l_ref[h] = alpha * l_ref[h] + jnp.sum(p, axis=0, keepdims=True)
        m_ref[h] = m_new
        vt = vt_ref[0, h * V_HEAD_DIM:(h + 1) * V_HEAD_DIM, pl.ds(off, tk)]
        pv = jnp.dot(vt, p.astype(vt.dtype), preferred_element_type=jnp.float32)
        acc_refs[h][...] = alpha * acc_refs[h][...] + pv

    def step(j, masked):
        off = pl.multiple_of(j * tk, tk)
        s_next = scores(0, off)
        for h in range(N_HEADS):
            s_cur = s_next
            if h + 1 < N_HEADS:
                s_next = scores(h + 1, off)
            softmax_pv(h, s_cur, off, masked)

    def body(j, c):
        step(j, False)
        return c

    lax.fori_loop(0, qi, body, 0)
    step(qi, True)

    f32 = jnp.float32
    lam = (jnp.exp(jnp.sum(lq1_ref[...].astype(f32) * lk1_ref[...].astype(f32), axis=-1, keepdims=True))
           - jnp.exp(jnp.sum(lq2_ref[...].astype(f32) * lk2_ref[...].astype(f32), axis=-1, keepdims=True))
           + LAM_INIT)
    for h in range(N_HEADS):
        a = acc_refs[h][...] / l_ref[h]
        o = a[:, :tq] - lam * a[:, tq:]
        on = o * lax.rsqrt(jnp.mean(o * o, axis=0, keepdims=True) + EPS)
        o_ref[0, :, h * V_HEAD_DIM:(h + 1) * V_HEAD_DIM] = (
            (on.T * gsub_ref[...]) * (1.0 - LAM_INIT)).astype(o_ref.dtype)


def _attn_call(qt, k, vt, lq1, lk1, lq2, lk2, gsub):
    B, _, S = qt.shape
    assert TQ == TK and TQ % CHUNK == 0
    const = lambda b, q: (0, 0)
    f32 = jnp.float32
    return pl.pallas_call(
        _attn_kernel,
        grid=(B, S // TQ),
        in_specs=[
            pl.BlockSpec((1, ATTN_QK, TQ), lambda b, q: (b, 0, q)),
            pl.BlockSpec((1, S, ATTN_QK), lambda b, q: (b, 0, 0)),
            pl.BlockSpec((1, ATTN_V, S), lambda b, q: (b, 0, 0)),
            pl.BlockSpec((1, HEAD_DIM), const),
            pl.BlockSpec((1, HEAD_DIM), const),
            pl.BlockSpec((1, HEAD_DIM), const),
            pl.BlockSpec((1, HEAD_DIM), const),
            pl.BlockSpec((1, V_HEAD_DIM), const),
        ],
        out_specs=pl.BlockSpec((1, TQ, ATTN_V), lambda b, q: (b, q, 0)),
        out_shape=jax.ShapeDtypeStruct((B, S, ATTN_V), jnp.bfloat16),
        scratch_shapes=[
            pltpu.VMEM((N_HEADS, 2 * HEAD_DIM, 2 * TQ), jnp.bfloat16),
            pltpu.VMEM((N_HEADS, 1, 2 * TQ), f32),
            pltpu.VMEM((N_HEADS, 1, 2 * TQ), f32),
        ] + [pltpu.VMEM((V_HEAD_DIM, 2 * TQ), f32) for _ in range(N_HEADS)],
        compiler_params=pltpu.CompilerParams(
            dimension_semantics=("arbitrary", "arbitrary"),
            vmem_limit_bytes=VMEM_LIMIT),
        name="diff_attn",
    )(qt, k, vt, lq1, lk1, lq2, lk2, gsub)


def _post_kernel(x_ref, conv_ref, o_ref, gmix_ref, wg_ref, bg_ref, wco_ref, wao_ref, wo_ref,
                 gffn_ref, wgu_ref, wd_ref, out_ref):
    f32 = jnp.float32
    bf = jnp.bfloat16
    x = x_ref[...]
    h = _rms_rows(x, gmix_ref[...]).astype(bf)
    gates = jax.nn.sigmoid(jnp.dot(h, wg_ref[...], preferred_element_type=f32) + bg_ref[...])
    y_c = jnp.dot(conv_ref[...], wco_ref[...], preferred_element_type=f32)
    y_a = jnp.dot(o_ref[...], wao_ref[...], preferred_element_type=f32)
    mix = (gates[:, :D_MODEL] * y_c + gates[:, D_MODEL:] * y_a).astype(bf)
    x1 = x + jnp.dot(mix, wo_ref[...], preferred_element_type=f32)
    h2 = _rms_rows(x1, gffn_ref[...]).astype(bf)
    acc = x1
    for lo, hi in FF_CHUNKS:
        gt = jnp.dot(h2, wgu_ref[:, lo:hi], preferred_element_type=f32)
        up = jnp.dot(h2, wgu_ref[:, D_FF + lo:D_FF + hi], preferred_element_type=f32)
        a = (gt * jax.nn.sigmoid(gt) * up).astype(bf)
        acc = acc + jnp.dot(a, wd_ref[lo:hi, :], preferred_element_type=f32)
    out_ref[...] = acc


def _post_call(x2, conv2, o2, g_mix, w_gates, b_gate, w_co, w_ao, w_o, g_ffn, w_gu, w_d):
    T, D = x2.shape
    ts = TS_POST
    row = lambda i: (i, 0)
    const = lambda i: (0, 0)

    def resident(shape):
        return pl.BlockSpec(shape, const, pipeline_mode=pl.Buffered(1))

    return pl.pallas_call(
        _post_kernel,
        grid=(T // ts,),
        in_specs=[
            pl.BlockSpec((ts, D), row),
            pl.BlockSpec((ts, D_CONV), row),
            pl.BlockSpec((ts, ATTN_V), row),
            resident((1, D)),
            resident((D, 2 * D)),
            resident((1, 2 * D)),
            resident((D_CONV, D)),
            resident((ATTN_V, D)),
            resident((D, D)),
            resident((1, D)),
            resident((D, 2 * D_FF)),
            resident((D_FF, D)),
        ],
        out_specs=pl.BlockSpec((ts, D), row),
        out_shape=jax.ShapeDtypeStruct((T, D), jnp.float32),
        compiler_params=pltpu.CompilerParams(
            dimension_semantics=("arbitrary",),
            vmem_limit_bytes=VMEM_LIMIT),
        name="post",
    )(x2, conv2, o2, g_mix, w_gates, b_gate, w_co, w_ao, w_o, g_ffn, w_gu, w_d)


def _rope_gain_tables(seq, gain, out_scale):
    pos = jnp.arange(seq, dtype=jnp.float32)
    inv = 1.0 / (ROPE_THETA ** (jnp.arange(0, HEAD_DIM, 2, dtype=jnp.float32) / HEAD_DIM))
    ang = inv[:, None] * pos[None, :]
    ang = jnp.concatenate([ang, ang], axis=0)
    g = gain.astype(jnp.float32)
    g_swap = jnp.concatenate([-g[HALF:], g[:HALF]])
    c_tab = jnp.cos(ang) * (g * out_scale)[:, None]
    s_tab = jnp.sin(ang) * (g_swap * out_scale)[:, None]
    return c_tab, s_tab


def kernel(x, g_mix, w_in, b_gate, conv_w, q_norm, k_norm, lambda_q1, lambda_k1, lambda_q2,
           lambda_k2, sub_norm, w_conv_out, w_attn_out, w_o, g_ffn, w_gate_up, w_down):
    B, S, D = x.shape
    bf = jnp.bfloat16
    l = 0
    q_scale = HEAD_DIM ** -0.5 * math.log2(math.e)
    cq, sq = _rope_gain_tables(S, q_norm[l], q_scale)
    ck, sk = _rope_gain_tables(S, k_norm[l], 1.0)

    w = w_in[l]
    w_conv_in = w[:, :CONV_COLS].astype(bf)
    w_qkv_t = w[:, CONV_COLS:CONV_COLS + QKV_COLS].T.astype(bf)
    w_gates = w[:, CONV_COLS + QKV_COLS:].astype(bf)

    conv_pre, qt, k, vt = _proj_call(x, g_mix[l][None], w_conv_in, w_qkv_t, conv_w[l], cq, sq, ck, sk)
    o = _attn_call(qt, k, vt, lambda_q1[l][None], lambda_k1[l][None], lambda_q2[l][None],
                   lambda_k2[l][None], sub_norm[l][None])
    out = _post_call(
        x.reshape(B * S, D), conv_pre.reshape(B * S, D_CONV), o.reshape(B * S, ATTN_V),
        g_mix[l][None], w_gates, b_gate[l][None], w_conv_out[l].astype(bf), w_attn_out[l].astype(bf),
        w_o[l].astype(bf), g_ffn[l][None], w_gate_up[l].astype(bf), w_down[l].astype(bf))
    return out.reshape(B, S, D)
```

```python
import functools
import math

import jax
import jax.numpy as jnp
from jax import lax
from jax.experimental import pallas as pl
from jax.experimental.pallas import tpu as pltpu

D_MODEL = 1024
CHUNK = 64
D_CONV = 512
CONV_K = 3
N_HEADS = 4
HEAD_DIM = 64
V_HEAD_DIM = 2 * HEAD_DIM
ATTN_QK = N_HEADS * 2 * HEAD_DIM
ATTN_V = N_HEADS * V_HEAD_DIM
D_FF = 2816
ROPE_THETA = 10000.0
EPS = 1e-6
LAM_INIT = 0.8 - 0.6 * math.exp(-0.3 * 0)

N_GROUPS = ATTN_QK // HEAD_DIM
HALF = HEAD_DIM // 2
QKV_COLS = ATTN_QK + ATTN_QK + ATTN_V
CONV_COLS = 3 * D_CONV

TS_PROJ = 512
TQ = 256
TK = 256
TS_POST = 512
FF_CHUNKS = ((0, 1024), (1024, 2048), (2048, 2816))
NEG_BIG = -1e30
VMEM_LIMIT = 56 * 1024 * 1024

_NT = (((1,), (1,)), ((), ()))


def _rms_rows(x, g):
    ms = jnp.mean(x * x, axis=-1, keepdims=True)
    return x * lax.rsqrt(ms + EPS) * g


def _norm_rope_t(z, c_tab, s_tab):
    r = lax.rsqrt(jnp.mean(z * z, axis=0, keepdims=True) + EPS)
    swap = jnp.concatenate([z[HALF:], z[:HALF]], axis=0)
    return (z * c_tab + swap * s_tab) * r


def _proj_kernel(x_ref, g_ref, wc_ref, wt_ref, cw_ref, cq_ref, sq_ref, ck_ref, sk_ref,
                 conv_ref, qt_ref, k_ref, vt_ref, carry_ref):
    s_idx = pl.program_id(1)
    ts = x_ref.shape[1]

    h = _rms_rows(x_ref[0], g_ref[...]).astype(jnp.bfloat16)

    zc = jnp.dot(h, wc_ref[...], preferred_element_type=jnp.float32)
    bc = zc[:, :D_CONV]
    u = zc[:, D_CONV:2 * D_CONV] * zc[:, 2 * D_CONV:]

    @pl.when(s_idx == 0)
    def _():
        carry_ref[...] = jnp.zeros_like(carry_ref)

    prev = carry_ref[...]
    row = lax.broadcasted_iota(jnp.int32, u.shape, 0)
    u1 = jnp.where(row == 0, prev[7:8], pltpu.roll(u, 1, 0))
    u2 = jnp.where(row == 0, prev[6:7],
                   jnp.where(row == 1, prev[7:8], pltpu.roll(u, 2, 0)))
    cw = cw_ref[...]
    y = cw[0:1] * u2 + cw[1:2] * u1 + cw[2:3] * u
    conv_ref[0] = (bc * y).astype(conv_ref.dtype)
    carry_ref[...] = u[ts - 8:]

    zt = lax.dot_general(wt_ref[...], h, _NT, preferred_element_type=jnp.float32)
    cq, sq, ck, sk = cq_ref[...], sq_ref[...], ck_ref[...], sk_ref[...]
    for g in range(N_GROUPS):
        lo = g * HEAD_DIM
        qt_ref[0, lo:lo + HEAD_DIM, :] = _norm_rope_t(zt[lo:lo + HEAD_DIM], cq, sq).astype(qt_ref.dtype)
    kt = jnp.concatenate(
        [_norm_rope_t(zt[ATTN_QK + g * HEAD_DIM:ATTN_QK + (g + 1) * HEAD_DIM], ck, sk)
         for g in range(N_GROUPS)], axis=0)
    k_ref[0] = kt.T.astype(k_ref.dtype)
    vt_ref[0] = zt[2 * ATTN_QK:].astype(vt_ref.dtype)


def _proj_call(x, g_mix, w_conv_in, w_qkv_t, conv_w, cq, sq, ck, sk):
    B, S, D = x.shape
    ts = TS_PROJ
    bf = jnp.bfloat16
    const = lambda b, s: (0, 0)
    return pl.pallas_call(
        _proj_kernel,
        grid=(B, S // ts),
        in_specs=[
            pl.BlockSpec((1, ts, D), lambda b, s: (b, s, 0)),
            pl.BlockSpec((1, D), const),
            pl.BlockSpec((D, CONV_COLS), const),
            pl.BlockSpec((QKV_COLS, D), const),
            pl.BlockSpec((CONV_K, D_CONV), const),
            pl.BlockSpec((HEAD_DIM, ts), lambda b, s: (0, s)),
            pl.BlockSpec((HEAD_DIM, ts), lambda b, s: (0, s)),
            pl.BlockSpec((HEAD_DIM, ts), lambda b, s: (0, s)),
            pl.BlockSpec((HEAD_DIM, ts), lambda b, s: (0, s)),
        ],
        out_specs=[
            pl.BlockSpec((1, ts, D_CONV), lambda b, s: (b, s, 0)),
            pl.BlockSpec((1, ATTN_QK, ts), lambda b, s: (b, 0, s)),
            pl.BlockSpec((1, ts, ATTN_QK), lambda b, s: (b, s, 0)),
            pl.BlockSpec((1, ATTN_V, ts), lambda b, s: (b, 0, s)),
        ],
        out_shape=[
            jax.ShapeDtypeStruct((B, S, D_CONV), bf),
            jax.ShapeDtypeStruct((B, ATTN_QK, S), bf),
            jax.ShapeDtypeStruct((B, S, ATTN_QK), bf),
            jax.ShapeDtypeStruct((B, ATTN_V, S), bf),
        ],
        scratch_shapes=[pltpu.VMEM((8, D_CONV), jnp.float32)],
        compiler_params=pltpu.CompilerParams(
            dimension_semantics=("arbitrary", "arbitrary"),
            vmem_limit_bytes=VMEM_LIMIT),
        name="proj",
    )(x, g_mix, w_conv_in, w_qkv_t, conv_w, cq, sq, ck, sk)


def _attn_kernel(qt_ref, k_ref, vt_ref, lq1_ref, lk1_ref, lq2_ref, lk2_ref, gsub_ref, o_ref,
                 qz_ref, m_ref, l_ref, *head_refs):
    acc_refs, s_refs = head_refs[:N_HEADS], head_refs[N_HEADS:]
    qi = pl.program_id(1)
    tq = o_ref.shape[1]
    tk = TK
    hw = 2 * HEAD_DIM

    for h in range(N_HEADS):
        qt = qt_ref[0, h * hw:(h + 1) * hw, :]
        sub_row = lax.broadcasted_iota(jnp.int32, qt.shape, 0)
        zero = jnp.zeros_like(qt)
        qz_ref[h] = jnp.concatenate(
            [jnp.where(sub_row < HEAD_DIM, qt, zero), jnp.where(sub_row >= HEAD_DIM, qt, zero)], axis=1)
        acc_refs[h][...] = jnp.zeros_like(acc_refs[h])
    m_ref[...] = jnp.full_like(m_ref, NEG_BIG)
    l_ref[...] = jnp.zeros_like(l_ref)

    def scores(h, off):
        kt = k_ref[0, pl.ds(off, tk), h * hw:(h + 1) * hw]
        s_refs[h][...] = jnp.dot(kt, qz_ref[h], preferred_element_type=jnp.float32)

    def softmax(h, masked):
        s = s_refs[h][...]
        if masked:
            kc = lax.broadcasted_iota(jnp.int32, s.shape, 0) // CHUNK
            qc = (lax.broadcasted_iota(jnp.int32, s.shape, 1) % tq) // CHUNK
            s = jnp.where(kc <= qc, s, NEG_BIG)
        m = m_ref[h]
        m_new = jnp.maximum(m, jnp.max(s, axis=0, keepdims=True))
        alpha = jnp.exp2(m - m_new)
        p = jnp.exp2(s - m_new)
        l_ref[h] = alpha * l_ref[h] + jnp.sum(p, axis=0, keepdims=True)
        m_ref[h] = m_new
        return alpha, p.astype(jnp.bfloat16)

    def pv_update(h, alpha, p, off):
        vt = vt_ref[0, h * V_HEAD_DIM:(h + 1) * V_HEAD_DIM, pl.ds(off, tk)]
        pv = jnp.dot(vt, p, preferred_element_type=jnp.float32)
        acc_refs[h][...] = alpha * acc_refs[h][...] + pv

    for h in range(N_HEADS):
        scores(h, 0)

    def body(j, c):
        off = pl.multiple_of(j * tk, tk)
        for h in range(N_HEADS):
            alpha, p = softmax(h, False)
            scores(h, off + tk)
            pv_update(h, alpha, p, off)
        return c

    lax.fori_loop(0, qi, body, 0)
    off = pl.multiple_of(qi * tk, tk)
    for h in range(N_HEADS):
        alpha, p = softmax(h, True)
        pv_update(h, alpha, p, off)

    f32 = jnp.float32
    lam = (jnp.exp(jnp.sum(lq1_ref[...].astype(f32) * lk1_ref[...].astype(f32), axis=-1, keepdims=True))
           - jnp.exp(jnp.sum(lq2_ref[...].astype(f32) * lk2_ref[...].astype(f32), axis=-1, keepdims=True))
           + LAM_INIT)
    for h in range(N_HEADS):
        a = acc_refs[h][...] / l_ref[h]
        o = a[:, :tq] - lam * a[:, tq:]
        on = o * lax.rsqrt(jnp.mean(o * o, axis=0, keepdims=True) + EPS)
        o_ref[0, :, h * V_HEAD_DIM:(h + 1) * V_HEAD_DIM] = (
            (on.T * gsub_ref[...]) * (1.0 - LAM_INIT)).astype(o_ref.dtype)


def _attn_call(qt, k, vt, lq1, lk1, lq2, lk2, gsub):
    B, _, S = qt.shape
    assert TQ == TK and TQ % CHUNK == 0
    const = lambda b, q: (0, 0)
    f32 = jnp.float32
    return pl.pallas_call(
        _attn_kernel,
        grid=(B, S // TQ),
        in_specs=[
            pl.BlockSpec((1, ATTN_QK, TQ), lambda b, q: (b, 0, q)),
            pl.BlockSpec((1, S, ATTN_QK), lambda b, q: (b, 0, 0)),
            pl.BlockSpec((1, ATTN_V, S), lambda b, q: (b, 0, 0)),
            pl.BlockSpec((1, HEAD_DIM), const),
            pl.BlockSpec((1, HEAD_DIM), const),
            pl.BlockSpec((1, HEAD_DIM), const),
            pl.BlockSpec((1, HEAD_DIM), const),
            pl.BlockSpec((1, V_HEAD_DIM), const),
        ],
        out_specs=pl.BlockSpec((1, TQ, ATTN_V), lambda b, q: (b, q, 0)),
        out_shape=jax.ShapeDtypeStruct((B, S, ATTN_V), jnp.bfloat16),
        scratch_shapes=[
            pltpu.VMEM((N_HEADS, 2 * HEAD_DIM, 2 * TQ), jnp.bfloat16),
            pltpu.VMEM((N_HEADS, 1, 2 * TQ), f32),
            pltpu.VMEM((N_HEADS, 1, 2 * TQ), f32),
        ] + [pltpu.VMEM((V_HEAD_DIM, 2 * TQ), f32) for _ in range(N_HEADS)]
          + [pltpu.VMEM((TK, 2 * TQ), f32) for _ in range(N_HEADS)],
        compiler_params=pltpu.CompilerParams(
            dimension_semantics=("arbitrary", "arbitrary"),
            vmem_limit_bytes=VMEM_LIMIT),
        name="diff_attn",
    )(qt, k, vt, lq1, lk1, lq2, lk2, gsub)


def _post_kernel(x_ref, conv_ref, o_ref, gmix_ref, wg_ref, bg_ref, wco_ref, wao_ref, wo_ref,
                 gffn_ref, wgu_ref, wd_ref, out_ref):
    f32 = jnp.float32
    bf = jnp.bfloat16
    x = x_ref[...]
    h = _rms_rows(x, gmix_ref[...]).astype(bf)
    gates = jax.nn.sigmoid(jnp.dot(h, wg_ref[...], preferred_element_type=f32) + bg_ref[...])
    y_c = jnp.dot(conv_ref[...], wco_ref[...], preferred_element_type=f32)
    y_a = jnp.dot(o_ref[...], wao_ref[...], preferred_element_type=f32)
    mix = (gates[:, :D_MODEL] * y_c + gates[:, D_MODEL:] * y_a).astype(bf)
    x1 = x + jnp.dot(mix, wo_ref[...], preferred_element_type=f32)
    h2 = _rms_rows(x1, gffn_ref[...]).astype(bf)
    acc = x1
    for lo, hi in FF_CHUNKS:
        gt = jnp.dot(h2, wgu_ref[:, lo:hi], preferred_element_type=f32)
        up = jnp.dot(h2, wgu_ref[:, D_FF + lo:D_FF + hi], preferred_element_type=f32)
        a = (gt * jax.nn.sigmoid(gt) * up).astype(bf)
        acc = acc + jnp.dot(a, wd_ref[lo:hi, :], preferred_element_type=f32)
    out_ref[...] = acc


def _post_call(x2, conv2, o2, g_mix, w_gates, b_gate, w_co, w_ao, w_o, g_ffn, w_gu, w_d):
    T, D = x2.shape
    ts = TS_POST
    row = lambda i: (i, 0)
    const = lambda i: (0, 0)

    def resident(shape):
        return pl.BlockSpec(shape, const, pipeline_mode=pl.Buffered(1))

    return pl.pallas_call(
        _post_kernel,
        grid=(T // ts,),
        in_specs=[
            pl.BlockSpec((ts, D), row),
            pl.BlockSpec((ts, D_CONV), row),
            pl.BlockSpec((ts, ATTN_V), row),
            resident((1, D)),
            resident((D, 2 * D)),
            resident((1, 2 * D)),
            resident((D_CONV, D)),
            resident((ATTN_V, D)),
            resident((D, D)),
            resident((1, D)),
            resident((D, 2 * D_FF)),
            resident((D_FF, D)),
        ],
        out_specs=pl.BlockSpec((ts, D), row),
        out_shape=jax.ShapeDtypeStruct((T, D), jnp.float32),
        compiler_params=pltpu.CompilerParams(
            dimension_semantics=("arbitrary",),
            vmem_limit_bytes=VMEM_LIMIT),
        name="post",
    )(x2, conv2, o2, g_mix, w_gates, b_gate, w_co, w_ao, w_o, g_ffn, w_gu, w_d)


def _rope_gain_tables(seq, gain, out_scale):
    pos = jnp.arange(seq, dtype=jnp.float32)
    inv = 1.0 / (ROPE_THETA ** (jnp.arange(0, HEAD_DIM, 2, dtype=jnp.float32) / HEAD_DIM))
    ang = inv[:, None] * pos[None, :]
    ang = jnp.concatenate([ang, ang], axis=0)
    g = gain.astype(jnp.float32)
    g_swap = jnp.concatenate([-g[HALF:], g[:HALF]])
    c_tab = jnp.cos(ang) * (g * out_scale)[:, None]
    s_tab = jnp.sin(ang) * (g_swap * out_scale)[:, None]
    return c_tab, s_tab


def kernel(x, g_mix, w_in, b_gate, conv_w, q_norm, k_norm, lambda_q1, lambda_k1, lambda_q2,
           lambda_k2, sub_norm, w_conv_out, w_attn_out, w_o, g_ffn, w_gate_up, w_down):
    B, S, D = x.shape
    bf = jnp.bfloat16
    l = 0
    q_scale = HEAD_DIM ** -0.5 * math.log2(math.e)
    cq, sq = _rope_gain_tables(S, q_norm[l], q_scale)
    ck, sk = _rope_gain_tables(S, k_norm[l], 1.0)

    w = w_in[l]
    w_conv_in = w[:, :CONV_COLS].astype(bf)
    w_qkv_t = w[:, CONV_COLS:CONV_COLS + QKV_COLS].T.astype(bf)
    w_gates = w[:, CONV_COLS + QKV_COLS:].astype(bf)

    conv_pre, qt, k, vt = _proj_call(x, g_mix[l][None], w_conv_in, w_qkv_t, conv_w[l], cq, sq, ck, sk)
    o = _attn_call(qt, k, vt, lambda_q1[l][None], lambda_k1[l][None], lambda_q2[l][None],
                   lambda_k2[l][None], sub_norm[l][None])
    out = _post_call(
        x.reshape(B * S, D), conv_pre.reshape(B * S, D_CONV), o.reshape(B * S, ATTN_V),
        g_mix[l][None], w_gates, b_gate[l][None], w_conv_out[l].astype(bf), w_attn_out[l].astype(bf),
        w_o[l].astype(bf), g_ffn[l][None], w_gate_up[l].astype(bf), w_down[l].astype(bf))
    return out.reshape(B, S, D)
```

```python
import functools
import math

import jax
import jax.numpy as jnp
from jax import lax
from jax.experimental import pallas as pl
from jax.experimental.pallas import tpu as pltpu

D_MODEL = 1024
CHUNK = 64
D_CONV = 512
CONV_K = 3
N_HEADS = 4
HEAD_DIM = 64
V_HEAD_DIM = 2 * HEAD_DIM
ATTN_QK = N_HEADS * 2 * HEAD_DIM
ATTN_V = N_HEADS * V_HEAD_DIM
D_FF = 2816
ROPE_THETA = 10000.0
EPS = 1e-6
LAM_INIT = 0.8 - 0.6 * math.exp(-0.3 * 0)

N_GROUPS = ATTN_QK // HEAD_DIM
HALF = HEAD_DIM // 2
QKV_COLS = ATTN_QK + ATTN_QK + ATTN_V
CONV_COLS = 3 * D_CONV

TS_PROJ = 512
TQ = 256
TK = 256
TS_POST = 512
FF_CHUNKS = ((0, 1024), (1024, 2048), (2048, 2816))
NEG_BIG = -1e30
SCORE_BOUND = 50.0
VMEM_LIMIT = 56 * 1024 * 1024

_NT = (((1,), (1,)), ((), ()))


def _rms_rows(x, g):
    ms = jnp.mean(x * x, axis=-1, keepdims=True)
    return x * lax.rsqrt(ms + EPS) * g


def _norm_rope_t(z, c_tab, s_tab):
    r = lax.rsqrt(jnp.mean(z * z, axis=0, keepdims=True) + EPS)
    swap = jnp.concatenate([z[HALF:], z[:HALF]], axis=0)
    return (z * c_tab + swap * s_tab) * r


def _proj_kernel(x_ref, g_ref, wc_ref, wt_ref, cw_ref, cq_ref, sq_ref, ck_ref, sk_ref,
                 conv_ref, qt_ref, k_ref, vt_ref, carry_ref):
    s_idx = pl.program_id(1)
    ts = x_ref.shape[1]

    h = _rms_rows(x_ref[0], g_ref[...]).astype(jnp.bfloat16)

    zc = jnp.dot(h, wc_ref[...], preferred_element_type=jnp.float32)
    bc = zc[:, :D_CONV]
    u = zc[:, D_CONV:2 * D_CONV] * zc[:, 2 * D_CONV:]

    @pl.when(s_idx == 0)
    def _():
        carry_ref[...] = jnp.zeros_like(carry_ref)

    prev = carry_ref[...]
    row = lax.broadcasted_iota(jnp.int32, u.shape, 0)
    u1 = jnp.where(row == 0, prev[7:8], pltpu.roll(u, 1, 0))
    u2 = jnp.where(row == 0, prev[6:7],
                   jnp.where(row == 1, prev[7:8], pltpu.roll(u, 2, 0)))
    cw = cw_ref[...]
    y = cw[0:1] * u2 + cw[1:2] * u1 + cw[2:3] * u
    conv_ref[0] = (bc * y).astype(conv_ref.dtype)
    carry_ref[...] = u[ts - 8:]

    zt = lax.dot_general(wt_ref[...], h, _NT, preferred_element_type=jnp.float32)
    cq, sq, ck, sk = cq_ref[...], sq_ref[...], ck_ref[...], sk_ref[...]
    for g in range(N_GROUPS):
        lo = g * HEAD_DIM
        qt_ref[0, lo:lo + HEAD_DIM, :] = _norm_rope_t(zt[lo:lo + HEAD_DIM], cq, sq).astype(qt_ref.dtype)
    kt = jnp.concatenate(
        [_norm_rope_t(zt[ATTN_QK + g * HEAD_DIM:ATTN_QK + (g + 1) * HEAD_DIM], ck, sk)
         for g in range(N_GROUPS)], axis=0)
    k_ref[0] = kt.T.astype(k_ref.dtype)
    vt_ref[0] = zt[2 * ATTN_QK:].astype(vt_ref.dtype)


def _proj_call(x, g_mix, w_conv_in, w_qkv_t, conv_w, cq, sq, ck, sk):
    B, S, D = x.shape
    ts = TS_PROJ
    bf = jnp.bfloat16
    const = lambda b, s: (0, 0)
    return pl.pallas_call(
        _proj_kernel,
        grid=(B, S // ts),
        in_specs=[
            pl.BlockSpec((1, ts, D), lambda b, s: (b, s, 0)),
            pl.BlockSpec((1, D), const),
            pl.BlockSpec((D, CONV_COLS), const),
            pl.BlockSpec((QKV_COLS, D), const),
            pl.BlockSpec((CONV_K, D_CONV), const),
            pl.BlockSpec((HEAD_DIM, ts), lambda b, s: (0, s)),
            pl.BlockSpec((HEAD_DIM, ts), lambda b, s: (0, s)),
            pl.BlockSpec((HEAD_DIM, ts), lambda b, s: (0, s)),
            pl.BlockSpec((HEAD_DIM, ts), lambda b, s: (0, s)),
        ],
        out_specs=[
            pl.BlockSpec((1, ts, D_CONV), lambda b, s: (b, s, 0)),
            pl.BlockSpec((1, ATTN_QK, ts), lambda b, s: (b, 0, s)),
            pl.BlockSpec((1, ts, ATTN_QK), lambda b, s: (b, s, 0)),
            pl.BlockSpec((1, ATTN_V, ts), lambda b, s: (b, 0, s)),
        ],
        out_shape=[
            jax.ShapeDtypeStruct((B, S, D_CONV), bf),
            jax.ShapeDtypeStruct((B, ATTN_QK, S), bf),
            jax.ShapeDtypeStruct((B, S, ATTN_QK), bf),
            jax.ShapeDtypeStruct((B, ATTN_V, S), bf),
        ],
        scratch_shapes=[pltpu.VMEM((8, D_CONV), jnp.float32)],
        compiler_params=pltpu.CompilerParams(
            dimension_semantics=("arbitrary", "arbitrary"),
            vmem_limit_bytes=VMEM_LIMIT),
        name="proj",
    )(x, g_mix, w_conv_in, w_qkv_t, conv_w, cq, sq, ck, sk)


def _attn_kernel(qt_ref, k_ref, vt_ref, lq1_ref, lk1_ref, lq2_ref, lk2_ref, gsub_ref, gk_ref, o_ref,
                 qz_ref, m_ref, l_ref, *head_refs):
    acc_refs, s_refs = head_refs[:N_HEADS], head_refs[N_HEADS:]
    qi = pl.program_id(1)
    tq = o_ref.shape[1]
    tk = TK
    hw = 2 * HEAD_DIM

    f32 = jnp.float32
    q_sq_max = jnp.zeros((1, 1), f32)
    for h in range(N_HEADS):
        qt = qt_ref[0, h * hw:(h + 1) * hw, :]
        sub_row = lax.broadcasted_iota(jnp.int32, qt.shape, 0)
        zero = jnp.zeros_like(qt)
        qz = jnp.concatenate(
            [jnp.where(sub_row < HEAD_DIM, qt, zero), jnp.where(sub_row >= HEAD_DIM, qt, zero)], axis=1)
        qz_ref[h] = qz
        qf = qz.astype(f32)
        q_sq = jnp.sum(qf * qf, axis=0, keepdims=True)
        q_sq_max = jnp.maximum(q_sq_max, jnp.max(q_sq, axis=1, keepdims=True))
        acc_refs[h][...] = jnp.zeros_like(acc_refs[h])
    l_ref[...] = jnp.zeros_like(l_ref)

    gk = gk_ref[...].astype(f32)
    k_sq_bound = HEAD_DIM * jnp.max(gk * gk, axis=1, keepdims=True) * 1.05
    bounded = (q_sq_max * k_sq_bound)[0, 0] < SCORE_BOUND * SCORE_BOUND

    def scores(h, off):
        kt = k_ref[0, pl.ds(off, tk), h * hw:(h + 1) * hw]
        s_refs[h][...] = jnp.dot(kt, qz_ref[h], preferred_element_type=jnp.float32)

    def masked_scores(h, masked):
        s = s_refs[h][...]
        if masked:
            kc = lax.broadcasted_iota(jnp.int32, s.shape, 0) // CHUNK
            qc = (lax.broadcasted_iota(jnp.int32, s.shape, 1) % tq) // CHUNK
            s = jnp.where(kc <= qc, s, NEG_BIG)
        return s

    def softmax_online(h, masked):
        s = masked_scores(h, masked)
        m = m_ref[h]
        m_new = jnp.maximum(m, jnp.max(s, axis=0, keepdims=True))
        alpha = jnp.exp2(m - m_new)
        p = jnp.exp2(s - m_new)
        l_ref[h] = alpha * l_ref[h] + jnp.sum(p, axis=0, keepdims=True)
        m_ref[h] = m_new
        return alpha, p.astype(jnp.bfloat16)

    def softmax_bounded(h, masked):
        p = jnp.exp2(masked_scores(h, masked))
        l_ref[h] = l_ref[h] + jnp.sum(p, axis=0, keepdims=True)
        return None, p.astype(jnp.bfloat16)

    def pv_update(h, alpha, p, off):
        vt = vt_ref[0, h * V_HEAD_DIM:(h + 1) * V_HEAD_DIM, pl.ds(off, tk)]
        pv = jnp.dot(vt, p, preferred_element_type=jnp.float32)
        acc = acc_refs[h][...]
        acc_refs[h][...] = (acc if alpha is None else alpha * acc) + pv

    def sweep(softmax):
        for h in range(N_HEADS):
            scores(h, 0)

        def body(j, c):
            off = pl.multiple_of(j * tk, tk)
            for h in range(N_HEADS):
                alpha, p = softmax(h, False)
                scores(h, off + tk)
                pv_update(h, alpha, p, off)
            return c

        lax.fori_loop(0, qi, body, 0)
        off = pl.multiple_of(qi * tk, tk)
        for h in range(N_HEADS):
            alpha, p = softmax(h, True)
            pv_update(h, alpha, p, off)

    @pl.when(bounded)
    def _():
        sweep(softmax_bounded)

    @pl.when(jnp.logical_not(bounded))
    def _():
        m_ref[...] = jnp.full_like(m_ref, NEG_BIG)
        sweep(softmax_online)

    lam = (jnp.exp(jnp.sum(lq1_ref[...].astype(f32) * lk1_ref[...].astype(f32), axis=-1, keepdims=True))
           - jnp.exp(jnp.sum(lq2_ref[...].astype(f32) * lk2_ref[...].astype(f32), axis=-1, keepdims=True))
           + LAM_INIT)
    for h in range(N_HEADS):
        a = acc_refs[h][...] / l_ref[h]
        o = a[:, :tq] - lam * a[:, tq:]
        on = o * lax.rsqrt(jnp.mean(o * o, axis=0, keepdims=True) + EPS)
        o_ref[0, :, h * V_HEAD_DIM:(h + 1) * V_HEAD_DIM] = (
            (on.T * gsub_ref[...]) * (1.0 - LAM_INIT)).astype(o_ref.dtype)


def _attn_call(qt, k, vt, lq1, lk1, lq2, lk2, gsub, gk):
    B, _, S = qt.shape
    assert TQ == TK and TQ % CHUNK == 0
    const = lambda b, q: (0, 0)
    f32 = jnp.float32
    return pl.pallas_call(
        _attn_kernel,
        grid=(B, S // TQ),
        in_specs=[
            pl.BlockSpec((1, ATTN_QK, TQ), lambda b, q: (b, 0, q)),
            pl.BlockSpec((1, S, ATTN_QK), lambda b, q: (b, 0, 0)),
            pl.BlockSpec((1, ATTN_V, S), lambda b, q: (b, 0, 0)),
            pl.BlockSpec((1, HEAD_DIM), const),
            pl.BlockSpec((1, HEAD_DIM), const),
            pl.BlockSpec((1, HEAD_DIM), const),
            pl.BlockSpec((1, HEAD_DIM), const),
            pl.BlockSpec((1, V_HEAD_DIM), const),
            pl.BlockSpec((1, HEAD_DIM), const),
        ],
        out_specs=pl.BlockSpec((1, TQ, ATTN_V), lambda b, q: (b, q, 0)),
        out_shape=jax.ShapeDtypeStruct((B, S, ATTN_V), jnp.bfloat16),
        scratch_shapes=[
            pltpu.VMEM((N_HEADS, 2 * HEAD_DIM, 2 * TQ), jnp.bfloat16),
            pltpu.VMEM((N_HEADS, 1, 2 * TQ), f32),
            pltpu.VMEM((N_HEADS, 1, 2 * TQ), f32),
        ] + [pltpu.VMEM((V_HEAD_DIM, 2 * TQ), f32) for _ in range(N_HEADS)]
          + [pltpu.VMEM((TK, 2 * TQ), f32) for _ in range(N_HEADS)],
        compiler_params=pltpu.CompilerParams(
            dimension_semantics=("arbitrary", "arbitrary"),
            vmem_limit_bytes=VMEM_LIMIT),
        name="diff_attn",
    )(qt, k, vt, lq1, lk1, lq2, lk2, gsub, gk)


def _post_kernel(x_ref, conv_ref, o_ref, gmix_ref, wg_ref, bg_ref, wco_ref, wao_ref, wo_ref,
                 gffn_ref, wgu_ref, wd_ref, out_ref):
    f32 = jnp.float32
    bf = jnp.bfloat16
    x = x_ref[...]
    h = _rms_rows(x, gmix_ref[...]).astype(bf)
    gates = jax.nn.sigmoid(jnp.dot(h, wg_ref[...], preferred_element_type=f32) + bg_ref[...])
    y_c = jnp.dot(conv_ref[...], wco_ref[...], preferred_element_type=f32)
    y_a = jnp.dot(o_ref[...], wao_ref[...], preferred_element_type=f32)
    mix = (gates[:, :D_MODEL] * y_c + gates[:, D_MODEL:] * y_a).astype(bf)
    x1 = x + jnp.dot(mix, wo_ref[...], preferred_element_type=f32)
    h2 = _rms_rows(x1, gffn_ref[...]).astype(bf)
    acc = x1
    for lo, hi in FF_CHUNKS:
        gt = jnp.dot(h2, wgu_ref[:, lo:hi], preferred_element_type=f32)
        up = jnp.dot(h2, wgu_ref[:, D_FF + lo:D_FF + hi], preferred_element_type=f32)
        a = (gt * jax.nn.sigmoid(gt) * up).astype(bf)
        acc = acc + jnp.dot(a, wd_ref[lo:hi, :], preferred_element_type=f32)
    out_ref[...] = acc


def _post_call(x2, conv2, o2, g_mix, w_gates, b_gate, w_co, w_ao, w_o, g_ffn, w_gu, w_d):
    T, D = x2.shape
    ts = TS_POST
    row = lambda i: (i, 0)
    const = lambda i: (0, 0)

    def resident(shape):
        return pl.BlockSpec(shape, const, pipeline_mode=pl.Buffered(1))

    return pl.pallas_call(
        _post_kernel,
        grid=(T // ts,),
        in_specs=[
            pl.BlockSpec((ts, D), row),
            pl.BlockSpec((ts, D_CONV), row),
            pl.BlockSpec((ts, ATTN_V), row),
            resident((1, D)),
            resident((D, 2 * D)),
            resident((1, 2 * D)),
            resident((D_CONV, D)),
            resident((ATTN_V, D)),
            resident((D, D)),
            resident((1, D)),
            resident((D, 2 * D_FF)),
            resident((D_FF, D)),
        ],
        out_specs=pl.BlockSpec((ts, D), row),
        out_shape=jax.ShapeDtypeStruct((T, D), jnp.float32),
        compiler_params=pltpu.CompilerParams(
            dimension_semantics=("arbitrary",),
            vmem_limit_bytes=VMEM_LIMIT),
        name="post",
    )(x2, conv2, o2, g_mix, w_gates, b_gate, w_co, w_ao, w_o, g_ffn, w_gu, w_d)


def _rope_gain_tables(seq, gain, out_scale):
    pos = jnp.arange(seq, dtype=jnp.float32)
    inv = 1.0 / (ROPE_THETA ** (jnp.arange(0, HEAD_DIM, 2, dtype=jnp.float32) / HEAD_DIM))
    ang = inv[:, None] * pos[None, :]
    ang = jnp.concatenate([ang, ang], axis=0)
    g = gain.astype(jnp.float32)
    g_swap = jnp.concatenate([-g[HALF:], g[:HALF]])
    c_tab = jnp.cos(ang) * (g * out_scale)[:, None]
    s_tab = jnp.sin(ang) * (g_swap * out_scale)[:, None]
    return c_tab, s_tab


def kernel(x, g_mix, w_in, b_gate, conv_w, q_norm, k_norm, lambda_q1, lambda_k1, lambda_q2,
           lambda_k2, sub_norm, w_conv_out, w_attn_out, w_o, g_ffn, w_gate_up, w_down):
    B, S, D = x.shape
    bf = jnp.bfloat16
    l = 0
    q_scale = HEAD_DIM ** -0.5 * math.log2(math.e)
    cq, sq = _rope_gain_tables(S, q_norm[l], q_scale)
    ck, sk = _rope_gain_tables(S, k_norm[l], 1.0)

    w = w_in[l]
    w_conv_in = w[:, :CONV_COLS].astype(bf)
    w_qkv_t = w[:, CONV_COLS:CONV_COLS + QKV_COLS].T.astype(bf)
    w_gates = w[:, CONV_COLS + QKV_COLS:].astype(bf)

    conv_pre, qt, k, vt = _proj_call(x, g_mix[l][None], w_conv_in, w_qkv_t, conv_w[l], cq, sq, ck, sk)
    o = _attn_call(qt, k, vt, lambda_q1[l][None], lambda_k1[l][None], lambda_q2[l][None],
                   lambda_k2[l][None], sub_norm[l][None], k_norm[l][None])
    out = _post_call(
        x.reshape(B * S, D), conv_pre.reshape(B * S, D_CONV), o.reshape(B * S, ATTN_V),
        g_mix[l][None], w_gates, b_gate[l][None], w_conv_out[l].astype(bf), w_attn_out[l].astype(bf),
        w_o[l].astype(bf), g_ffn[l][None], w_gate_up[l].astype(bf), w_down[l].astype(bf))
    return out.reshape(B, S, D)
```

```python
import math

import jax
import jax.numpy as jnp
from jax import lax
from jax.experimental import pallas as pl
from jax.experimental.pallas import tpu as pltpu

D_MODEL = 1024
CHUNK = 64
D_CONV = 512
CONV_K = 3
N_HEADS = 4
HEAD_DIM = 64
V_HEAD_DIM = 2 * HEAD_DIM
ATTN_QK = N_HEADS * 2 * HEAD_DIM
ATTN_V = N_HEADS * V_HEAD_DIM
D_FF = 2816
ROPE_THETA = 10000.0
EPS = 1e-6
LAM_INIT = 0.8 - 0.6 * math.exp(-0.3 * 0)

N_GROUPS = ATTN_QK // HEAD_DIM
HALF = HEAD_DIM // 2
QKV_COLS = ATTN_QK + ATTN_QK + ATTN_V
CONV_COLS = 3 * D_CONV

TS_PROJ = 512
TK = 256
TQ = 2 * TK
HALF_LANES = 2 * TK
SCORE_LANES = 2 * TQ
TS_POST = 512
FF_CHUNKS = ((0, 1024), (1024, 2048), (2048, 2816))
NEG_BIG = -1e30
SCORE_BOUND = 50.0
Q_SCALE = HEAD_DIM ** -0.5 * math.log2(math.e)
VMEM_LIMIT = 56 * 1024 * 1024

_NT = (((1,), (1,)), ((), ()))


def _rms_rows(x, g):
    ms = jnp.mean(x * x, axis=-1, keepdims=True)
    return x * lax.rsqrt(ms + EPS) * g


def _norm_rope_t(z, c_tab, s_tab):
    r = lax.rsqrt(jnp.mean(z * z, axis=0, keepdims=True) + EPS)
    swap = jnp.concatenate([z[HALF:], z[:HALF]], axis=0)
    return (z * c_tab + swap * s_tab) * r


def _proj_kernel(x_ref, g_ref, wc_ref, wt_ref, cw_ref, cq_ref, sq_ref, ck_ref, sk_ref,
                 conv_ref, qt_ref, k_ref, vt_ref, carry_ref):
    s_idx = pl.program_id(1)
    ts = x_ref.shape[1]

    h = _rms_rows(x_ref[0], g_ref[...]).astype(jnp.bfloat16)

    zc = jnp.dot(h, wc_ref[...], preferred_element_type=jnp.float32)
    bc = zc[:, :D_CONV]
    u = zc[:, D_CONV:2 * D_CONV] * zc[:, 2 * D_CONV:]

    @pl.when(s_idx == 0)
    def _():
        carry_ref[...] = jnp.zeros_like(carry_ref)

    prev = carry_ref[...]
    row = lax.broadcasted_iota(jnp.int32, u.shape, 0)
    u1 = jnp.where(row == 0, prev[7:8], pltpu.roll(u, 1, 0))
    u2 = jnp.where(row == 0, prev[6:7],
                   jnp.where(row == 1, prev[7:8], pltpu.roll(u, 2, 0)))
    cw = cw_ref[...]
    y = cw[0:1] * u2 + cw[1:2] * u1 + cw[2:3] * u
    conv_ref[0] = (bc * y).astype(conv_ref.dtype)
    carry_ref[...] = u[ts - 8:]

    zt = lax.dot_general(wt_ref[...], h, _NT, preferred_element_type=jnp.float32)
    cq, sq, ck, sk = cq_ref[...], sq_ref[...], ck_ref[...], sk_ref[...]
    for g in range(N_GROUPS):
        lo = g * HEAD_DIM
        qt_ref[0, lo:lo + HEAD_DIM, :] = _norm_rope_t(zt[lo:lo + HEAD_DIM], cq, sq).astype(qt_ref.dtype)
    kt = jnp.concatenate(
        [_norm_rope_t(zt[ATTN_QK + g * HEAD_DIM:ATTN_QK + (g + 1) * HEAD_DIM], ck, sk)
         for g in range(N_GROUPS)], axis=0)
    k_ref[0] = kt.T.astype(k_ref.dtype)
    vt_ref[0] = zt[2 * ATTN_QK:].astype(vt_ref.dtype)


def _proj_call(x, g_mix, w_conv_in, w_qkv_t, conv_w, cq, sq, ck, sk):
    B, S, D = x.shape
    ts = TS_PROJ
    bf = jnp.bfloat16
    const = lambda b, s: (0, 0)
    return pl.pallas_call(
        _proj_kernel,
        grid=(B, S // ts),
        in_specs=[
            pl.BlockSpec((1, ts, D), lambda b, s: (b, s, 0)),
            pl.BlockSpec((1, D), const),
            pl.BlockSpec((D, CONV_COLS), const),
            pl.BlockSpec((QKV_COLS, D), const),
            pl.BlockSpec((CONV_K, D_CONV), const),
            pl.BlockSpec((HEAD_DIM, ts), lambda b, s: (0, s)),
            pl.BlockSpec((HEAD_DIM, ts), lambda b, s: (0, s)),
            pl.BlockSpec((HEAD_DIM, ts), lambda b, s: (0, s)),
            pl.BlockSpec((HEAD_DIM, ts), lambda b, s: (0, s)),
        ],
        out_specs=[
            pl.BlockSpec((1, ts, D_CONV), lambda b, s: (b, s, 0)),
            pl.BlockSpec((1, ATTN_QK, ts), lambda b, s: (b, 0, s)),
            pl.BlockSpec((1, ts, ATTN_QK), lambda b, s: (b, s, 0)),
            pl.BlockSpec((1, ATTN_V, ts), lambda b, s: (b, 0, s)),
        ],
        out_shape=[
            jax.ShapeDtypeStruct((B, S, D_CONV), bf),
            jax.ShapeDtypeStruct((B, ATTN_QK, S), bf),
            jax.ShapeDtypeStruct((B, S, ATTN_QK), bf),
            jax.ShapeDtypeStruct((B, ATTN_V, S), bf),
        ],
        scratch_shapes=[pltpu.VMEM((8, D_CONV), jnp.float32)],
        compiler_params=pltpu.CompilerParams(
            dimension_semantics=("arbitrary", "arbitrary"),
            vmem_limit_bytes=VMEM_LIMIT),
        name="proj",
    )(x, g_mix, w_conv_in, w_qkv_t, conv_w, cq, sq, ck, sk)


def _attn_kernel(qt_ref, k_ref, vt_ref, lq1_ref, lk1_ref, lq2_ref, lk2_ref, gsub_ref, gq_ref, gk_ref, o_ref,
                 qz_ref, m_ref, l_ref, bias_ref, *head_refs):
    acc_refs, s_refs = head_refs[:N_HEADS], head_refs[N_HEADS:]
    qi = pl.program_id(1)
    tk = TK
    hw = 2 * HEAD_DIM
    f32 = jnp.float32
    full = (0, SCORE_LANES)
    second_half = (HALF_LANES, HALF_LANES)

    def scores(h, off, lanes):
        lo, n = lanes
        kt = k_ref[0, pl.ds(off, tk), h * hw:(h + 1) * hw]
        s_refs[h][:, lo:lo + n] = jnp.dot(kt, qz_ref[h, :, lo:lo + n],
                                          preferred_element_type=f32)

    for h in range(N_HEADS):
        qt = qt_ref[0, h * hw:(h + 1) * hw, :]
        sub_row = lax.broadcasted_iota(jnp.int32, (hw, TK), 0)
        zero = jnp.zeros((hw, TK), qt.dtype)
        pieces = []
        for g in range(TQ // TK):
            qg = qt[:, g * TK:(g + 1) * TK]
            pieces += [jnp.where(sub_row < HEAD_DIM, qg, zero), jnp.where(sub_row >= HEAD_DIM, qg, zero)]
        qz_ref[h] = jnp.concatenate(pieces, axis=1)
    for h in range(N_HEADS):
        scores(h, 0, full)
    for h in range(N_HEADS):
        acc_refs[h][...] = jnp.zeros_like(acc_refs[h])
    l_ref[...] = jnp.zeros_like(l_ref)

    @pl.when(jnp.logical_and(pl.program_id(0) == 0, qi == 0))
    def _():
        kc = lax.broadcasted_iota(jnp.int32, bias_ref.shape, 0) // CHUNK
        qc = (lax.broadcasted_iota(jnp.int32, bias_ref.shape, 1) % TK) // CHUNK
        bias_ref[...] = jnp.where(kc <= qc, 0.0, NEG_BIG).astype(f32)

    gq = gq_ref[...].astype(f32)
    gk = gk_ref[...].astype(f32)
    sq_bound = ((HEAD_DIM * Q_SCALE) ** 2 * 1.05) * (jnp.max(gq * gq, axis=1, keepdims=True)
                                                    * jnp.max(gk * gk, axis=1, keepdims=True))
    bounded = sq_bound[0, 0] < SCORE_BOUND * SCORE_BOUND

    def masked_scores(h, lanes, mask):
        lo, n = lanes
        if mask is None:
            return s_refs[h][:, lo:lo + n]
        lo_m = 0 if mask == "first" else HALF_LANES
        s_diag = s_refs[h][:, lo_m:lo_m + HALF_LANES] + bias_ref[...]
        if mask == "first":
            return jnp.concatenate([s_diag, s_refs[h][:, HALF_LANES:]], axis=1)
        return s_diag

    def softmax_online(h, lanes, mask):
        lo, n = lanes
        s = masked_scores(h, lanes, mask)
        m = m_ref[h, :, lo:lo + n]
        m_new = jnp.maximum(m, jnp.max(s, axis=0, keepdims=True))
        alpha = jnp.exp2(m - m_new)
        p = jnp.exp2(s - m_new)
        l_ref[h, :, lo:lo + n] = alpha * l_ref[h, :, lo:lo + n] + jnp.sum(p, axis=0, keepdims=True)
        m_ref[h, :, lo:lo + n] = m_new
        return alpha, p.astype(jnp.bfloat16)

    def softmax_bounded(h, lanes, mask):
        lo, n = lanes
        p = jnp.exp2(masked_scores(h, lanes, mask))
        l_ref[h, :, lo:lo + n] = l_ref[h, :, lo:lo + n] + jnp.sum(p, axis=0, keepdims=True)
        return None, p.astype(jnp.bfloat16)

    def pv_update(h, alpha, p, off, lanes):
        lo, n = lanes
        vt = vt_ref[0, h * V_HEAD_DIM:(h + 1) * V_HEAD_DIM, pl.ds(off, tk)]
        pv = jnp.dot(vt, p, preferred_element_type=f32)
        acc = acc_refs[h][:, lo:lo + n]
        acc_refs[h][:, lo:lo + n] = (acc if alpha is None else alpha * acc) + pv

    def sweep(softmax):
        def body(j, c):
            off = pl.multiple_of(j * tk, tk)
            for h in range(N_HEADS):
                alpha, p = softmax(h, full, None)
                scores(h, off + tk, full)
                pv_update(h, alpha, p, off, full)
            return c

        n_full = (TQ // TK) * qi
        lax.fori_loop(0, n_full, body, 0)
        off = pl.multiple_of(n_full * tk, tk)
        for h in range(N_HEADS):
            alpha, p = softmax(h, full, "first")
            scores(h, off + tk, second_half)
            pv_update(h, alpha, p, off, full)
        for h in range(N_HEADS):
            alpha, p = softmax(h, second_half, "second")
            pv_update(h, alpha, p, off + tk, second_half)

    @pl.when(bounded)
    def _():
        sweep(softmax_bounded)

    @pl.when(jnp.logical_not(bounded))
    def _():
        m_ref[...] = jnp.full_like(m_ref, NEG_BIG)
        sweep(softmax_online)

    lam = (jnp.exp(jnp.sum(lq1_ref[...].astype(f32) * lk1_ref[...].astype(f32), axis=-1, keepdims=True))
           - jnp.exp(jnp.sum(lq2_ref[...].astype(f32) * lk2_ref[...].astype(f32), axis=-1, keepdims=True))
           + LAM_INIT)
    out_gain = gsub_ref[...] * (1.0 - LAM_INIT)
    for h in range(N_HEADS):
        inv_l = 1.0 / l_ref[h]
        for g in range(TQ // TK):
            lo = g * HALF_LANES
            o = (acc_refs[h][:, lo:lo + TK] * inv_l[:, lo:lo + TK]
                 - acc_refs[h][:, lo + TK:lo + 2 * TK] * (lam * inv_l[:, lo + TK:lo + 2 * TK]))
            on = o * lax.rsqrt(jnp.mean(o * o, axis=0, keepdims=True) + EPS)
            o_ref[0, g * TK:(g + 1) * TK, h * V_HEAD_DIM:(h + 1) * V_HEAD_DIM] = (
                on.T * out_gain).astype(o_ref.dtype)


def _attn_call(qt, k, vt, lq1, lk1, lq2, lk2, gsub, gq, gk):
    B, _, S = qt.shape
    assert TQ == 2 * TK and TK % CHUNK == 0 and S % TQ == 0
    const = lambda b, q: (0, 0)
    f32 = jnp.float32
    return pl.pallas_call(
        _attn_kernel,
        grid=(B, S // TQ),
        in_specs=[
            pl.BlockSpec((1, ATTN_QK, TQ), lambda b, q: (b, 0, q)),
            pl.BlockSpec((1, S, ATTN_QK), lambda b, q: (b, 0, 0)),
            pl.BlockSpec((1, ATTN_V, S), lambda b, q: (b, 0, 0)),
            pl.BlockSpec((1, HEAD_DIM), const),
            pl.BlockSpec((1, HEAD_DIM), const),
            pl.BlockSpec((1, HEAD_DIM), const),
            pl.BlockSpec((1, HEAD_DIM), const),
            pl.BlockSpec((1, V_HEAD_DIM), const),
            pl.BlockSpec((1, HEAD_DIM), const),
            pl.BlockSpec((1, HEAD_DIM), const),
        ],
        out_specs=pl.BlockSpec((1, TQ, ATTN_V), lambda b, q: (b, q, 0)),
        out_shape=jax.ShapeDtypeStruct((B, S, ATTN_V), jnp.bfloat16),
        scratch_shapes=[
            pltpu.VMEM((N_HEADS, 2 * HEAD_DIM, SCORE_LANES), jnp.bfloat16),
            pltpu.VMEM((N_HEADS, 1, SCORE_LANES), f32),
            pltpu.VMEM((N_HEADS, 1, SCORE_LANES), f32),
            pltpu.VMEM((TK, HALF_LANES), f32),
        ] + [pltpu.VMEM((V_HEAD_DIM, SCORE_LANES), f32) for _ in range(N_HEADS)]
          + [pltpu.VMEM((TK, SCORE_LANES), f32) for _ in range(N_HEADS)],
        compiler_params=pltpu.CompilerParams(
            dimension_semantics=("arbitrary", "arbitrary"),
            vmem_limit_bytes=VMEM_LIMIT),
        name="diff_attn",
    )(qt, k, vt, lq1, lk1, lq2, lk2, gsub, gq, gk)


def _post_kernel(x_ref, conv_ref, o_ref, gmix_ref, wg_ref, bg_ref, wco_ref, wao_ref, wo_ref,
                 gffn_ref, wgu_ref, wd_ref, out_ref):
    f32 = jnp.float32
    bf = jnp.bfloat16
    x = x_ref[...]
    h = _rms_rows(x, gmix_ref[...]).astype(bf)
    gates = jax.nn.sigmoid(jnp.dot(h, wg_ref[...], preferred_element_type=f32) + bg_ref[...])
    y_c = jnp.dot(conv_ref[...], wco_ref[...], preferred_element_type=f32)
    y_a = jnp.dot(o_ref[...], wao_ref[...], preferred_element_type=f32)
    mix = (gates[:, :D_MODEL] * y_c + gates[:, D_MODEL:] * y_a).astype(bf)
    x1 = x + jnp.dot(mix, wo_ref[...], preferred_element_type=f32)
    h2 = _rms_rows(x1, gffn_ref[...]).astype(bf)
    acc = x1
    for lo, hi in FF_CHUNKS:
        gt = jnp.dot(h2, wgu_ref[:, lo:hi], preferred_element_type=f32)
        up = jnp.dot(h2, wgu_ref[:, D_FF + lo:D_FF + hi], preferred_element_type=f32)
        a = (gt * jax.nn.sigmoid(gt) * up).astype(bf)
        acc = acc + jnp.dot(a, wd_ref[lo:hi, :], preferred_element_type=f32)
    out_ref[...] = acc


def _post_call(x2, conv2, o2, g_mix, w_gates, b_gate, w_co, w_ao, w_o, g_ffn, w_gu, w_d):
    T, D = x2.shape
    ts = TS_POST
    row = lambda i: (i, 0)
    const = lambda i: (0, 0)

    def resident(shape):
        return pl.BlockSpec(shape, const, pipeline_mode=pl.Buffered(1))

    return pl.pallas_call(
        _post_kernel,
        grid=(T // ts,),
        in_specs=[
            pl.BlockSpec((ts, D), row),
            pl.BlockSpec((ts, D_CONV), row),
            pl.BlockSpec((ts, ATTN_V), row),
            resident((1, D)),
            resident((D, 2 * D)),
            resident((1, 2 * D)),
            resident((D_CONV, D)),
            resident((ATTN_V, D)),
            resident((D, D)),
            resident((1, D)),
            resident((D, 2 * D_FF)),
            resident((D_FF, D)),
        ],
        out_specs=pl.BlockSpec((ts, D), row),
        out_shape=jax.ShapeDtypeStruct((T, D), jnp.float32),
        compiler_params=pltpu.CompilerParams(
            dimension_semantics=("arbitrary",),
            vmem_limit_bytes=VMEM_LIMIT),
        name="post",
    )(x2, conv2, o2, g_mix, w_gates, b_gate, w_co, w_ao, w_o, g_ffn, w_gu, w_d)


def _rope_gain_tables(seq, gain, out_scale):
    pos = jnp.arange(seq, dtype=jnp.float32)
    inv = 1.0 / (ROPE_THETA ** (jnp.arange(0, HEAD_DIM, 2, dtype=jnp.float32) / HEAD_DIM))
    ang = inv[:, None] * pos[None, :]
    ang = jnp.concatenate([ang, ang], axis=0)
    g = gain.astype(jnp.float32)
    g_swap = jnp.concatenate([-g[HALF:], g[:HALF]])
    c_tab = jnp.cos(ang) * (g * out_scale)[:, None]
    s_tab = jnp.sin(ang) * (g_swap * out_scale)[:, None]
    return c_tab, s_tab


def kernel(x, g_mix, w_in, b_gate, conv_w, q_norm, k_norm, lambda_q1, lambda_k1, lambda_q2,
           lambda_k2, sub_norm, w_conv_out, w_attn_out, w_o, g_ffn, w_gate_up, w_down):
    B, S, D = x.shape
    bf = jnp.bfloat16
    l = 0
    cq, sq = _rope_gain_tables(S, q_norm[l], Q_SCALE)
    ck, sk = _rope_gain_tables(S, k_norm[l], 1.0)

    w = w_in[l]
    w_conv_in = w[:, :CONV_COLS].astype(bf)
    w_qkv_t = w[:, CONV_COLS:CONV_COLS + QKV_COLS].T.astype(bf)
    w_gates = w[:, CONV_COLS + QKV_COLS:].astype(bf)

    conv_pre, qt, k, vt = _proj_call(x, g_mix[l][None], w_conv_in, w_qkv_t, conv_w[l], cq, sq, ck, sk)
    o = _attn_call(qt, k, vt, lambda_q1[l][None], lambda_k1[l][None], lambda_q2[l][None],
                   lambda_k2[l][None], sub_norm[l][None], q_norm[l][None], k_norm[l][None])
    out = _post_call(
        x.reshape(B * S, D), conv_pre.reshape(B * S, D_CONV), o.reshape(B * S, ATTN_V),
        g_mix[l][None], w_gates, b_gate[l][None], w_conv_out[l].astype(bf), w_attn_out[l].astype(bf),
        w_o[l].astype(bf), g_ffn[l][None], w_gate_up[l].astype(bf), w_down[l].astype(bf))
    return out.reshape(B, S, D)
```
